```python
import math
import jax, jax.numpy as jnp
from jax import lax
import numpy as np

D_MODEL = 2048
BATCH = 16
SEQ = 256
DEPTH = 1
DEC_BATCH = 4
DEC_SEQ = 1024
PAST_LEN = 256

GRID_W = 64
H_A = 8
QK_A = 64
V_A = 2 * QK_A
H_B = 8
DK_B = 128
DV_B = 128
CONV_W = 3
CHUNK = 64
Q_BLOCK = 128
D_FF = 5632
ROPE_THETA = 10000.0
EPS = 1e-6
ATTN_Q = H_A * 2 * QK_A
ATTN_V = H_A * V_A
DN_QK = H_B * DK_B
DN_V = H_B * DV_B
DN_QKV = 2 * DN_QK + DN_V
MIX_W = ATTN_V + DN_V
N_IN = 2 * ATTN_Q + ATTN_V + DN_QKV + DN_V + 4 * H_B
SPLIT_IDX = [ATTN_Q, 2 * ATTN_Q, 2 * ATTN_Q + ATTN_V, 2 * ATTN_Q + ATTN_V + DN_QKV,
             2 * ATTN_Q + ATTN_V + DN_QKV + DN_V, 2 * ATTN_Q + ATTN_V + DN_QKV + DN_V + 2 * H_B]

kernel_name = "hymba_diffattn_gdn_prefix_dit"


def rmsnorm(x, g):
    xf = x.astype(jnp.float32)
    y = xf * lax.rsqrt(jnp.mean(xf * xf, axis=-1, keepdims=True) + EPS)
    return (y * g.astype(jnp.float32)).astype(x.dtype)


def l2norm(x):
    xf = x.astype(jnp.float32)
    return xf * lax.rsqrt(jnp.sum(xf * xf, axis=-1, keepdims=True) + EPS)


def modulation(cvec, w_mod, b_mod):
    m = jax.nn.silu(cvec) @ w_mod + b_mod
    return jnp.split(m[..., None, :], 6, axis=-1)


def dwconv_centred(x, w):
    k = w.shape[0]
    p = k // 2
    t = x.shape[1]
    xp = jnp.pad(x, ((0, 0), (p, p), (0, 0)))
    return sum(xp[:, i:i + t] * w[i] for i in range(k))


def axial_angles(t):
    rows = t // GRID_W
    r = jnp.repeat(jnp.arange(rows, dtype=jnp.float32), GRID_W)
    col = jnp.tile(jnp.arange(GRID_W, dtype=jnp.float32), rows)
    half = QK_A // 2
    inv = ROPE_THETA ** (-jnp.arange(0, half, 2, dtype=jnp.float32) / half)
    return r[:, None] * inv, col[:, None] * inv


def rotate_half_pairs(x, ang):
    x1, x2 = jnp.split(x, 2, axis=-1)
    cos, sin = jnp.cos(ang), jnp.sin(ang)
    return jnp.concatenate([x1 * cos - x2 * sin, x1 * sin + x2 * cos], axis=-1)


def apply_axial_rope(x, ang_r, ang_c):
    xf = x.astype(jnp.float32)
    xr, xc = jnp.split(xf, 2, axis=-1)
    bc = lambda a: a[None, :, None, None, :]
    return jnp.concatenate([rotate_half_pairs(xr, bc(ang_r)), rotate_half_pairs(xc, bc(ang_c))], axis=-1).astype(x.dtype)


def diff_attention(q1, q2, k1, k2, v, lam):
    b, t, h, d = q1.shape
    nb = t // Q_BLOCK
    scale = d ** -0.5
    k1f, k2f, vf = k1.astype(jnp.float32), k2.astype(jnp.float32), v.astype(jnp.float32)

    def block(qs):
        qa, qb = qs
        s1 = jnp.einsum('bqhd,bkhd->bhqk', qa.astype(jnp.float32), k1f) * scale
        s2 = jnp.einsum('bqhd,bkhd->bhqk', qb.astype(jnp.float32), k2f) * scale
        p = jax.nn.softmax(s1, axis=-1) - lam * jax.nn.softmax(s2, axis=-1)
        return jnp.einsum('bhqk,bkhd->bqhd', p, vf)

    split = lambda x: jnp.moveaxis(x.reshape(b, nb, Q_BLOCK, h, d), 1, 0)
    o = lax.map(block, (split(q1), split(q2)))
    return jnp.moveaxis(o, 0, 1).reshape(b, t, h, v.shape[-1])


def gated_delta_chunked(q, k, v, g, beta, s0):
    b, t, h, dk = q.shape
    dv = v.shape[-1]
    n = t // CHUNK

    def chunks(x):
        return jnp.moveaxis(x.reshape((b, n, CHUNK, h) + x.shape[3:]), 3, 1)

    q, k, v, g, beta = chunks(q), chunks(k), chunks(v), chunks(g), chunks(beta)
    gc = jnp.cumsum(g, axis=-1)
    incl = jnp.tril(jnp.ones((CHUNK, CHUNK), bool))
    strict = jnp.tril(jnp.ones((CHUNK, CHUNK), bool), -1)
    gamma = jnp.exp(jnp.where(incl, gc[..., :, None] - gc[..., None, :], -jnp.inf))
    kb = k * beta[..., None]
    a_mat = jnp.where(strict, jnp.einsum('bhncd,bhnsd->bhncs', kb, k) * gamma, 0.0)
    t_sys = a_mat + jnp.eye(CHUNK, dtype=jnp.float32)
    u = lax.linalg.triangular_solve(t_sys, v * beta[..., None], left_side=True, lower=True, unit_diagonal=True)
    w = lax.linalg.triangular_solve(t_sys, kb * jnp.exp(gc)[..., None], left_side=True, lower=True, unit_diagonal=True)
    qk = jnp.einsum('bhncd,bhnsd->bhncs', q, k) * gamma
    qg = q * jnp.exp(gc)[..., None]
    g_last = gc[..., -1]
    kd = k * jnp.exp(g_last[..., None] - gc)[..., None]
    dl = jnp.exp(g_last)

    def step(s, xs):
        u_n, w_n, qg_n, qk_n, kd_n, dl_n = xs
        v_new = u_n - jnp.einsum('bhcd,bhde->bhce', w_n, s)
        o_n = jnp.einsum('bhcd,bhde->bhce', qg_n, s) + jnp.einsum('bhcs,bhse->bhce', qk_n, v_new)
        s = s * dl_n[..., None, None] + jnp.einsum('bhcd,bhce->bhde', kd_n, v_new)
        return s, o_n

    xs = tuple(jnp.moveaxis(x, 2, 0) for x in (u, w, qg, qk, kd, dl))
    s_final, o = lax.scan(step, s0.astype(jnp.float32), xs)
    o = jnp.moveaxis(jnp.moveaxis(o, 0, 2), 1, 3).reshape(b, t, h, dv)
    return o, s_final


def bidir_delta(q, k, v, g, beta, s_f0, s_b0):
    flip = lambda x: jnp.flip(x, axis=1)
    o_f, s_f = gated_delta_chunked(q, k, v, g[:, :, 0], beta[:, :, 0], s_f0)
    o_b, s_b = gated_delta_chunked(flip(q), flip(k), flip(v), flip(g[:, :, 1]), flip(beta[:, :, 1]), s_b0)
    return o_f + flip(o_b), s_f, s_b


def mixer(h, w_in, conv_qkv_w, lam, lam_init, subln_g, a_log, dt_bias, dn_norm_g, w_out, ctx):
    b, t, _ = h.shape
    z = h @ w_in
    aq, ak, av, dqkv, dg, db, da = jnp.split(z, SPLIT_IDX, axis=-1)
    q = aq.reshape(b, t, H_A, 2, QK_A)
    k = ak.reshape(b, t, H_A, 2, QK_A)
    v = av.reshape(b, t, H_A, V_A)
    if ctx is None:
        keys, vals = k, v
        s_f0 = jnp.zeros((b, H_B, DK_B, DV_B), jnp.float32)
        s_b0 = s_f0
    else:
        ctx_k, ctx_v, s_f0, s_b0 = ctx
        ang_r, ang_c = axial_angles(t)
        q = apply_axial_rope(q, ang_r, ang_c)
        k = apply_axial_rope(k, ang_r, ang_c)
        keys = jnp.concatenate([k, ctx_k.reshape(b, ctx_k.shape[1], H_A, 2, QK_A).astype(k.dtype)], axis=1)
        vals = jnp.concatenate([v, ctx_v.astype(v.dtype)], axis=1)
    o_a = diff_attention(q[..., 0, :], q[..., 1, :], keys[..., 0, :], keys[..., 1, :], vals, lam)
    o_a = rmsnorm(o_a, subln_g) * (1.0 - lam_init)

    qkv = jax.nn.silu(dwconv_centred(dqkv, conv_qkv_w))
    dq, dk_, dv_ = jnp.split(qkv, [DN_QK, 2 * DN_QK], axis=-1)
    qd = l2norm(dq.reshape(b, t, H_B, DK_B)) * (DK_B ** -0.5)
    kd = l2norm(dk_.reshape(b, t, H_B, DK_B))
    vd = dv_.reshape(b, t, H_B, DV_B).astype(jnp.float32)
    beta = jax.nn.sigmoid(db.reshape(b, t, 2, H_B).astype(jnp.float32))
    gdec = -jnp.exp(a_log.astype(jnp.float32)) * jax.nn.softplus(
        da.reshape(b, t, 2, H_B).astype(jnp.float32) + dt_bias.astype(jnp.float32))
    o_d, s_f, s_b = bidir_delta(qd, kd, vd, gdec, beta, s_f0, s_b0)
    o_d = rmsnorm(o_d, dn_norm_g) * jax.nn.silu(dg.reshape(b, t, H_B, DV_B).astype(jnp.float32))

    o = jnp.concatenate([o_a.reshape(b, t, ATTN_V), o_d.reshape(b, t, DN_V)], axis=-1).astype(h.dtype)
    out = o @ w_out
    if ctx is None:
        return out, (ak.reshape(b, t, H_A, 2 * QK_A), v, s_f, s_b)
    return out, None


def conv_ffn(h, w_up, conv_w, w_down):
    u = dwconv_centred(h @ w_up, conv_w)
    gate, val = jnp.split(u, 2, axis=-1)
    return (jax.nn.silu(gate) * val) @ w_down


def trunk_layer(x, cvec, l, mod_w, mod_b, norm_mix_g, w_in, conv_qkv_w, lambda_q1, lambda_k1, lambda_q2,
                lambda_k2, subln_g, a_log, dt_bias, dn_norm_g, w_out, norm_ffn_g, w_up, conv_ffn_w, w_down, ctx):
    sh1, sc1, g1, sh2, sc2, g2 = modulation(cvec, mod_w[l], mod_b[l])
    lam_init = 0.8 - 0.6 * math.exp(-0.3 * l)
    lam = (jnp.exp(jnp.sum(lambda_q1[l].astype(jnp.float32) * lambda_k1[l].astype(jnp.float32)))
           - jnp.exp(jnp.sum(lambda_q2[l].astype(jnp.float32) * lambda_k2[l].astype(jnp.float32))) + lam_init)
    h = rmsnorm(x, norm_mix_g[l]) * (1 + sc1) + sh1
    m, new_ctx = mixer(h, w_in[l], conv_qkv_w[l], lam, lam_init, subln_g[l], a_log[l], dt_bias[l],
                       dn_norm_g[l], w_out[l], ctx)
    x = x + g1 * m
    h = rmsnorm(x, norm_ffn_g[l]) * (1 + sc2) + sh2
    x = x + g2 * conv_ffn(h, w_up[l], conv_ffn_w[l], w_down[l])
    return x, new_ctx


def setup_inputs(seed: int = 0) -> dict:
    key = jax.random.key(seed)
    ks = jax.random.split(key, 32)
    nrm = lambda i, shape, s=1.0: jax.random.normal(ks[i], shape, jnp.float32) * s
    centre = jnp.zeros((CONV_W, 1), jnp.float32).at[CONV_W // 2].set(1.0)
    dt = jnp.exp(jax.random.uniform(ks[26], (DEPTH, 2, H_B), jnp.float32, math.log(1e-3), math.log(1e-1)))
    return {
        "x_prompt": nrm(0, (BATCH, SEQ, D_MODEL)),
        "x_sample": nrm(1, (DEC_BATCH, DEC_SEQ, D_MODEL)),
        "cache_k": nrm(2, (DEC_BATCH, DEPTH, PAST_LEN, H_A, 2 * QK_A)),
        "cache_v": nrm(3, (DEC_BATCH, DEPTH, PAST_LEN, H_A, V_A)),
        "state_fwd": nrm(4, (DEC_BATCH, DEPTH, H_B, DK_B, DV_B), 0.1),
        "state_bwd": nrm(5, (DEC_BATCH, DEPTH, H_B, DK_B, DV_B), 0.1),
        "c": nrm(6, (DEC_BATCH, D_MODEL)),
        "c_ctx": nrm(7, (D_MODEL,)),
        "mod_w": nrm(8, (DEPTH, D_MODEL, 6 * D_MODEL), 0.5 * D_MODEL ** -0.5),
        "mod_b": nrm(9, (DEPTH, 6 * D_MODEL), 0.01),
        "norm_mix_g": 1.0 + nrm(10, (DEPTH, D_MODEL), 0.01),
        "w_in": nrm(11, (DEPTH, D_MODEL, N_IN), D_MODEL ** -0.5),
        "conv_qkv_w": centre + nrm(12, (DEPTH, CONV_W, DN_QKV), 0.2),
        "lambda_q1": nrm(13, (DEPTH, QK_A), 0.1),
        "lambda_k1": nrm(14, (DEPTH, QK_A), 0.1),
        "lambda_q2": nrm(15, (DEPTH, QK_A), 0.1),
        "lambda_k2": nrm(16, (DEPTH, QK_A), 0.1),
        "subln_g": 1.0 + nrm(17, (DEPTH, V_A), 0.01),
        "a_log": jnp.log(jax.random.uniform(ks[18], (DEPTH, 2, H_B), jnp.float32, 1.0, 16.0)),
        "dt_bias": dt + jnp.log(-jnp.expm1(-dt)),
        "dn_norm_g": 1.0 + nrm(19, (DEPTH, DV_B), 0.01),
        "w_out": nrm(20, (DEPTH, MIX_W, D_MODEL), MIX_W ** -0.5),
        "norm_ffn_g": 1.0 + nrm(21, (DEPTH, D_MODEL), 0.01),
        "w_up": nrm(22, (DEPTH, D_MODEL, 2 * D_FF), D_MODEL ** -0.5),
        "conv_ffn_w": centre + nrm(23, (DEPTH, CONV_W, 2 * D_FF), 0.2),
        "w_down": nrm(24, (DEPTH, D_FF, D_MODEL), D_FF ** -0.5),
        "final_g": 1.0 + nrm(25, (D_MODEL,), 0.01),
    }


def reference(x_prompt, x_sample, cache_k, cache_v, state_fwd, state_bwd, c, c_ctx, mod_w, mod_b, norm_mix_g,
              w_in, conv_qkv_w, lambda_q1, lambda_k1, lambda_q2, lambda_k2, subln_g, a_log, dt_bias, dn_norm_g,
              w_out, norm_ffn_g, w_up, conv_ffn_w, w_down, final_g):
    xp, xs = x_prompt, x_sample
    new_k, new_v, new_sf, new_sb = [], [], [], []
    for l in range(DEPTH):
        xp, (k_l, v_l, sf_l, sb_l) = trunk_layer(
            xp, c_ctx, l, mod_w, mod_b, norm_mix_g, w_in, conv_qkv_w, lambda_q1, lambda_k1, lambda_q2, lambda_k2,
            subln_g, a_log, dt_bias, dn_norm_g, w_out, norm_ffn_g, w_up, conv_ffn_w, w_down, None)
        new_k.append(k_l)
        new_v.append(v_l)
        new_sf.append(sf_l)
        new_sb.append(sb_l)
        xs, _ = trunk_layer(
            xs, c, l, mod_w, mod_b, norm_mix_g, w_in, conv_qkv_w, lambda_q1, lambda_k1, lambda_q2, lambda_k2,
            subln_g, a_log, dt_bias, dn_norm_g, w_out, norm_ffn_g, w_up, conv_ffn_w, w_down,
            (cache_k[:, l], cache_v[:, l], state_fwd[:, l], state_bwd[:, l]))
    y_prompt = rmsnorm(xp, final_g)
    y_sample = rmsnorm(xs, final_g)
    new_cache_k = jnp.stack(new_k, axis=1)
    new_cache_v = jnp.stack(new_v, axis=1)
    new_state_fwd = jnp.stack(new_sf, axis=1)
    new_state_bwd = jnp.stack(new_sb, axis=1)
    return (y_prompt, y_sample, new_cache_k, new_cache_v, new_state_fwd, new_state_bwd)
```

```python
import functools
import math

import jax
import jax.numpy as jnp
from jax import lax
from jax.experimental import pallas as pl
from jax.experimental.pallas import tpu as pltpu

F32 = jnp.float32
BF16 = jnp.bfloat16

D_MODEL = 2048
GRID_W = 64
H_A = 8
QK_A = 64
V_A = 2 * QK_A
H_B = 8
DK_B = 128
DV_B = 128
CHUNK = 64
D_FF = 5632
ROPE_THETA = 10000.0
EPS = 1e-6
ATTN_Q = H_A * 2 * QK_A
ATTN_V = H_A * V_A
DN_QK = H_B * DK_B
DN_V = H_B * DV_B
N_MAIN = 2 * ATTN_Q + ATTN_V + 2 * DN_QK + DN_V + DN_V
N_TAIL = 4 * H_B
LAM_INIT = 0.8 - 0.6 * math.exp(-0.3 * 0)

LANES = 128
VMEM_LIMIT = 56 * 1024 * 1024

TM = 1024
TN_IN = 512
TN_OUT = 512
TF = 256
ROW_CHUNK = 256
QB = 256


def _silu(x):
    return x / (1.0 + jnp.exp(-x))


def _dot(a, b):
    return jnp.dot(a, b, preferred_element_type=F32)


def _dot_nt(a, b):
    return lax.dot_general(a, b, (((1,), (1,)), ((), ())), preferred_element_type=F32)


def _dot_tn(a, b):
    return lax.dot_general(a, b, (((0,), (0,)), ((), ())), preferred_element_type=F32)


def _mod_kernel(c_ref, w_ref, b_ref, o_ref):
    s = _silu(c_ref[...])
    o_ref[...] = _dot(s.astype(BF16), w_ref[...].astype(BF16)) + b_ref[...]


def _modulation(cvecs, mod_w, mod_b):
    rows = cvecs.shape[0]
    n = mod_w.shape[1]
    tn = 1024
    return pl.pallas_call(
        _mod_kernel,
        grid=(n // tn,),
        in_specs=[
            pl.BlockSpec((rows, D_MODEL), lambda j: (0, 0)),
            pl.BlockSpec((D_MODEL, tn), lambda j: (0, j)),
            pl.BlockSpec((1, tn), lambda j: (0, j)),
        ],
        out_specs=pl.BlockSpec((rows, tn), lambda j: (0, j)),
        out_shape=jax.ShapeDtypeStruct((rows, n), F32),
        compiler_params=pltpu.CompilerParams(
            dimension_semantics=("arbitrary",), vmem_limit_bytes=VMEM_LIMIT),
        name="modulation",
    )(cvecs, mod_w, mod_b)


def _norm_mod_rows(x_ref, g_ref, sc_ref, sh_ref, h_scr):
    g = g_ref[...]
    a = 1.0 + sc_ref[0]
    b = sh_ref[0]

    def body(r, carry):
        rs = pl.multiple_of(r * ROW_CHUNK, ROW_CHUNK)
        x = x_ref[pl.ds(rs, ROW_CHUNK), :]
        ms = jnp.mean(x * x, axis=-1, keepdims=True)
        y = x * lax.rsqrt(ms + EPS) * g
        h_scr[pl.ds(rs, ROW_CHUNK), :] = (y * a + b).astype(BF16)
        return carry

    lax.fori_loop(0, x_ref.shape[0] // ROW_CHUNK, body, 0)


def _in_kernel(x_ref, g_ref, sc_ref, sh_ref, w_ref, wt_ref, z_ref, zs_ref, h_scr):
    @pl.when(pl.program_id(1) == 0)
    def _():
        _norm_mod_rows(x_ref, g_ref, sc_ref, sh_ref, h_scr)
        zs_ref[...] = _dot(h_scr[...], wt_ref[...].astype(BF16))

    z_ref[...] = _dot(h_scr[...], w_ref[...].astype(BF16))


def _in_proj(x, g, sc, sh, w_in, w_tail):
    tokens = x.shape[0]
    return pl.pallas_call(
        _in_kernel,
        grid=(tokens // TM, N_MAIN // TN_IN),
        in_specs=[
            pl.BlockSpec((TM, D_MODEL), lambda i, j: (i, 0)),
            pl.BlockSpec((1, D_MODEL), lambda i, j: (0, 0)),
            pl.BlockSpec((1, 1, D_MODEL), lambda i, j: (i, 0, 0)),
            pl.BlockSpec((1, 1, D_MODEL), lambda i, j: (i, 0, 0)),
            pl.BlockSpec((D_MODEL, TN_IN), lambda i, j: (0, j)),
            pl.BlockSpec((D_MODEL, N_TAIL), lambda i, j: (0, 0)),
        ],
        out_specs=[
            pl.BlockSpec((TM, TN_IN), lambda i, j: (i, j)),
            pl.BlockSpec((TM, N_TAIL), lambda i, j: (i, 0)),
        ],
        out_shape=[
            jax.ShapeDtypeStruct((tokens, N_MAIN), F32),
            jax.ShapeDtypeStruct((tokens, N_TAIL), F32),
        ],
        scratch_shapes=[pltpu.VMEM((TM, D_MODEL), BF16)],
        compiler_params=pltpu.CompilerParams(
            dimension_semantics=("arbitrary", "arbitrary"), vmem_limit_bytes=VMEM_LIMIT),
        name="in_proj",
    )(x, g, sc, sh, w_in, w_tail)


def _attn_kernel(*refs, has_ctx, seq):
    if has_ctx:
        (q_ref, k_ref, v_ref, ck_ref, cv_ref, cos_ref, sa_ref, sb_ref,
         lq1_ref, lk1_ref, lq2_ref, lk2_ref, sg_ref, o_ref, kf_scr, vf_scr) = refs
    else:
        (q_ref, k_ref, v_ref, lq1_ref, lk1_ref, lq2_ref, lk2_ref, sg_ref, o_ref) = refs

    lam = (jnp.exp(jnp.sum(lq1_ref[...] * lk1_ref[...], axis=-1, keepdims=True))
           - jnp.exp(jnp.sum(lq2_ref[...] * lk2_ref[...], axis=-1, keepdims=True)) + LAM_INIT)
    first_map = lax.broadcasted_iota(jnp.int32, (1, LANES), 1) < QK_A
    scale = QK_A ** -0.5
    out_gain = sg_ref[...] * (1.0 - LAM_INIT)

    def rope(x, rows):
        return (x * cos_ref[rows, :] + pltpu.roll(x, LANES - 16, 1) * sa_ref[rows, :]
                + pltpu.roll(x, 16, 1) * sb_ref[rows, :])

    if has_ctx:
        past = ck_ref.shape[1]
        kf_scr[0:seq, :] = rope(k_ref[...], slice(None)).astype(BF16)
        kf_scr[seq:seq + past, :] = ck_ref[0].astype(BF16)
        vf_scr[0:seq, :] = v_ref[...].astype(BF16)
        vf_scr[seq:seq + past, :] = cv_ref[0].astype(BF16)
        keys = kf_scr[...]
        vals = vf_scr[...]
    else:
        keys = k_ref[...].astype(BF16)
        vals = v_ref[...].astype(BF16)

    def softmax(s):
        e = jnp.exp(s - jnp.max(s, axis=-1, keepdims=True))
        return e * (1.0 / jnp.sum(e, axis=-1, keepdims=True))

    for qb in range(seq // QB):
        rows = slice(qb * QB, (qb + 1) * QB)
        q = q_ref[rows, :]
        if has_ctx:
            q = rope(q, rows)
        q = q * scale
        q1 = jnp.where(first_map, q, 0.0).astype(BF16)
        q2 = jnp.where(first_map, 0.0, q).astype(BF16)
        p = softmax(_dot_nt(q1, keys)) - lam * softmax(_dot_nt(q2, keys))
        o = _dot(p.astype(BF16), vals)
        o = o * lax.rsqrt(jnp.mean(o * o, axis=-1, keepdims=True) + EPS) * out_gain
        o_ref[rows, :] = o.astype(BF16)


def _attention(z, batch, seq, lams, subln_g, ctx=None):
    tokens = z.shape[0]
    q_off, k_off, v_off = 0, ATTN_Q // LANES, 2 * ATTN_Q // LANES
    in_specs = [
        pl.BlockSpec((seq, LANES), lambda b, h: (b, q_off + h)),
        pl.BlockSpec((seq, LANES), lambda b, h: (b, k_off + h)),
        pl.BlockSpec((seq, LANES), lambda b, h: (b, v_off + h)),
    ]
    args = [z, z, z]
    scratch = []
    if ctx is not None:
        ck, cv, cos_t, sa_t, sb_t = ctx
        past = ck.shape[1]
        in_specs += [
            pl.BlockSpec((1, past, LANES), lambda b, h: (b, 0, h)),
            pl.BlockSpec((1, past, LANES), lambda b, h: (b, 0, h)),
            pl.BlockSpec((seq, LANES), lambda b, h: (0, 0)),
            pl.BlockSpec((seq, LANES), lambda b, h: (0, 0)),
            pl.BlockSpec((seq, LANES), lambda b, h: (0, 0)),
        ]
        args += [ck, cv, cos_t, sa_t, sb_t]
        scratch = [pltpu.VMEM((seq + past, LANES), BF16), pltpu.VMEM((seq + past, LANES), BF16)]
    in_specs += [pl.BlockSpec((1, QK_A), lambda b, h: (0, 0))] * 4
    in_specs += [pl.BlockSpec((1, V_A), lambda b, h: (0, 0))]
    args += list(lams) + [subln_g]
    return pl.pallas_call(
        functools.partial(_attn_kernel, has_ctx=ctx is not None, seq=seq),
        grid=(batch, H_A),
        in_specs=in_specs,
        out_specs=pl.BlockSpec((seq, LANES), lambda b, h: (b, h)),
        out_shape=jax.ShapeDtypeStruct((tokens, ATTN_V), BF16),
        scratch_shapes=scratch,
        compiler_params=pltpu.CompilerParams(
            dimension_semantics=("arbitrary", "arbitrary"), vmem_limit_bytes=VMEM_LIMIT),
        name="diff_attention",
    )(*args)


def _rope_tables(seq):
    t = jnp.arange(seq, dtype=jnp.int32)
    r = (t // GRID_W).astype(F32)
    c = (t % GRID_W).astype(F32)
    half = QK_A // 2
    inv = ROPE_THETA ** (-jnp.arange(0, half, 2, dtype=F32) / half)
    ang_r = r[:, None] * inv
    ang_c = c[:, None] * inv
    ang = jnp.concatenate([ang_r, ang_r, ang_c, ang_c] * 2, axis=-1)
    lane = jnp.arange(LANES)
    is_x1 = (lane % 32) < 16
    cos_t = jnp.cos(ang)
    sin_t = jnp.sin(ang)
    sa_t = jnp.where(is_x1, -sin_t, 0.0)
    sb_t = jnp.where(is_x1, 0.0, sin_t)
    return cos_t, sa_t, sb_t


def _inv_unit_triangular(a):
    row = lax.broadcasted_iota(jnp.int32, (CHUNK, CHUNK), 0)
    col = lax.broadcasted_iota(jnp.int32, (CHUNK, CHUNK), 1)
    p = jnp.where(row == col, 1.0, 0.0) - a
    ak = a
    for _ in range(int(math.log2(CHUNK)) - 1):
        akb = ak.astype(BF16)
        ak = _dot(akb, akb)
        p = p + _dot(p.astype(BF16), ak.astype(BF16))
    return p


def _dn_chunk(q, k, v, g_row, beta_row, state, forward):
    row = lax.broadcasted_iota(jnp.int32, (CHUNK, CHUNK), 0)
    col = lax.broadcasted_iota(jnp.int32, (CHUNK, CHUNK), 1)
    if forward:
        incl, strict = col <= row, col < row
    else:
        incl, strict = col >= row, col > row
    eye = row == col
    lhs = jnp.concatenate([jnp.where(incl, g_row, 0.0), jnp.where(eye, beta_row, 0.0)], axis=0)
    rhs = jnp.concatenate([jnp.ones((CHUNK, LANES), F32), jnp.where(strict, 1.0, 0.0)], axis=1)
    sums = jnp.dot(lhs, rhs, precision=lax.Precision.HIGHEST, preferred_element_type=F32)
    gc = sums[:CHUNK, :LANES]
    beta = sums[CHUNK:, :LANES]
    dmat = sums[:CHUNK, LANES:]
    gamma = jnp.where(incl, jnp.exp(dmat), 0.0)
    g_last = gc[CHUNK - 1:CHUNK, :] if forward else gc[0:1, :]
    eg = jnp.exp(gc)
    kb = k * beta
    vb = v * beta
    kbg = kb * eg
    qg = q * eg
    kd = k * jnp.exp(g_last - gc)
    dl = jnp.exp(g_last)

    kq = _dot_nt(jnp.concatenate([kb, q], axis=0).astype(BF16), k.astype(BF16))
    a_mat = jnp.where(strict, kq[:CHUNK] * gamma, 0.0)
    qk = kq[CHUNK:] * gamma
    t_inv = _inv_unit_triangular(a_mat)
    uw = _dot(t_inv.astype(BF16), jnp.concatenate([vb, kbg], axis=1).astype(BF16))
    uwb = uw.astype(BF16)
    qk_uw = _dot(qk.astype(BF16), uwb)
    q_eff = qg - qk_uw[:, LANES:]
    o_loc = qk_uw[:, :LANES]
    kd_uw = _dot_tn(kd.astype(BF16), uwb)
    b_mat = kd_uw[:, :LANES]
    g_mat = kd_uw[:, LANES:]
    qs = _dot(jnp.concatenate([q_eff, g_mat], axis=0).astype(BF16), state.astype(BF16))
    o = qs[:CHUNK] + o_loc
    new_state = state * dl - qs[CHUNK:] + b_mat
    return o, new_state


def _dn_kernel(*refs, seq, has_init):
    if has_init:
        (zq_ref, zk_ref, zv_ref, zg_ref, cwq_ref, cwk_ref, cwv_ref, sm_ref, alog_ref, dtb_ref, ng_ref,
         s0f_ref, s0b_ref, o_ref, q_scr, k_scr, v_scr, of_scr, ob_scr) = refs
    else:
        (zq_ref, zk_ref, zv_ref, zg_ref, cwq_ref, cwk_ref, cwv_ref, sm_ref, alog_ref, dtb_ref, ng_ref,
         o_ref, sf_ref, sb_ref, q_scr, k_scr, v_scr, of_scr, ob_scr) = refs
    h = pl.program_id(1)
    n_chunks = seq // CHUNK
    row = lax.broadcasted_iota(jnp.int32, (seq, LANES), 0)

    def conv_silu(z_ref, cw_ref):
        x = z_ref[...]
        x_prev = jnp.where(row == 0, 0.0, pltpu.roll(x, 1, 0))
        x_next = jnp.where(row == seq - 1, 0.0, pltpu.roll(x, seq - 1, 0))
        cw = cw_ref[...]
        return _silu(x_prev * cw[0:1] + x * cw[1:2] + x_next * cw[2:3])

    def l2norm(x):
        return x * lax.rsqrt(jnp.sum(x * x, axis=-1, keepdims=True) + EPS)

    q_scr[...] = l2norm(conv_silu(zq_ref, cwq_ref)) * (DK_B ** -0.5)
    k_scr[...] = l2norm(conv_silu(zk_ref, cwk_ref))
    v_scr[...] = conv_silu(zv_ref, cwv_ref)

    def gate_rows(n, direction):
        beta = 1.0 / (1.0 + jnp.exp(-sm_ref[0, 0, direction, pl.ds(n, 1), :]))
        x = sm_ref[0, 0, 2 + direction, pl.ds(n, 1), :] + dtb_ref[direction, h]
        softplus = jnp.maximum(x, 0.0) + jnp.log(1.0 + jnp.exp(-jnp.abs(x)))
        neg_a = -jnp.exp(jnp.full((1, CHUNK), alog_ref[direction, h], F32))
        return neg_a * softplus, beta

    def step(i, carry):
        s_f, s_b = carry
        nf = i
        nb = n_chunks - 1 - i
        rf = pl.ds(pl.multiple_of(nf * CHUNK, CHUNK), CHUNK)
        rb = pl.ds(pl.multiple_of(nb * CHUNK, CHUNK), CHUNK)
        g_f, beta_f = gate_rows(nf, 0)
        g_b, beta_b = gate_rows(nb, 1)
        o_f, s_f = _dn_chunk(q_scr[rf, :], k_scr[rf, :], v_scr[rf, :], g_f, beta_f, s_f, True)
        o_b, s_b = _dn_chunk(q_scr[rb, :], k_scr[rb, :], v_scr[rb, :], g_b, beta_b, s_b, False)
        of_scr[rf, :] = o_f
        ob_scr[rb, :] = o_b
        return s_f, s_b

    if has_init:
        init = (s0f_ref[0, 0], s0b_ref[0, 0])
    else:
        init = (jnp.zeros((DK_B, DV_B), F32), jnp.zeros((DK_B, DV_B), F32))
    s_f, s_b = lax.fori_loop(0, n_chunks, step, init)
    if not has_init:
        sf_ref[0, 0] = s_f
        sb_ref[0, 0] = s_b

    o = of_scr[...] + ob_scr[...]
    o = o * lax.rsqrt(jnp.mean(o * o, axis=-1, keepdims=True) + EPS) * ng_ref[...]
    o_ref[...] = (o * _silu(zg_ref[...])).astype(BF16)


def _deltanet(z, zs, batch, seq, conv_w, a_log, dt_bias, norm_g, init=None):
    tokens = z.shape[0]
    n_chunks = seq // CHUNK
    sm = zs.reshape(batch, n_chunks, CHUNK, 2, 2, H_B).transpose(0, 5, 3, 4, 1, 2)
    sm = sm.reshape(batch, H_B, 4, n_chunks, CHUNK)
    q_off = (2 * ATTN_Q + ATTN_V) // LANES
    k_off = q_off + DN_QK // LANES
    v_off = k_off + DN_QK // LANES
    g_off = v_off + DN_V // LANES
    in_specs = [
        pl.BlockSpec((seq, LANES), lambda b, h: (b, q_off + h)),
        pl.BlockSpec((seq, LANES), lambda b, h: (b, k_off + h)),
        pl.BlockSpec((seq, LANES), lambda b, h: (b, v_off + h)),
        pl.BlockSpec((seq, LANES), lambda b, h: (b, g_off + h)),
        pl.BlockSpec((3, LANES), lambda b, h: (0, h)),
        pl.BlockSpec((3, LANES), lambda b, h: (0, DN_QK // LANES + h)),
        pl.BlockSpec((3, LANES), lambda b, h: (0, 2 * DN_QK // LANES + h)),
        pl.BlockSpec((1, 1, 4, n_chunks, CHUNK), lambda b, h: (b, h, 0, 0, 0)),
        pl.BlockSpec(memory_space=pltpu.SMEM),
        pl.BlockSpec(memory_space=pltpu.SMEM),
        pl.BlockSpec((1, DV_B), lambda b, h: (0, 0)),
    ]
    args = [z, z, z, z, conv_w, conv_w, conv_w, sm, a_log, dt_bias, norm_g]
    o_spec = pl.BlockSpec((seq, LANES), lambda b, h: (b, h))
    o_shape = jax.ShapeDtypeStruct((tokens, DN_V), BF16)
    state_spec = pl.BlockSpec((1, 1, DK_B, DV_B), lambda b, h: (b, h, 0, 0))
    if init is not None:
        in_specs += [state_spec, state_spec]
        args += list(init)
        out_specs, out_shape = o_spec, o_shape
    else:
        st_shape = jax.ShapeDtypeStruct((batch, H_B, DK_B, DV_B), F32)
        out_specs, out_shape = [o_spec, state_spec, state_spec], [o_shape, st_shape, st_shape]
    return pl.pallas_call(
        functools.partial(_dn_kernel, seq=seq, has_init=init is not None),
        grid=(batch, H_B),
        in_specs=in_specs,
        out_specs=out_specs,
        out_shape=out_shape,
        scratch_shapes=[pltpu.VMEM((seq, LANES), F32)] * 5,
        compiler_params=pltpu.CompilerParams(
            dimension_semantics=("arbitrary", "arbitrary"), vmem_limit_bytes=VMEM_LIMIT),
        name="gated_deltanet",
    )(*args)


def _out_kernel(oa_ref, od_ref, wa_ref, wd_ref, x_ref, gate_ref, o_ref):
    acc = _dot(oa_ref[...], wa_ref[...].astype(BF16)) + _dot(od_ref[...], wd_ref[...].astype(BF16))
    o_ref[...] = x_ref[...] + gate_ref[0] * acc


def _out_proj(o_a, o_d, w_out, x, gate):
    tokens = x.shape[0]
    return pl.pallas_call(
        _out_kernel,
        grid=(tokens // TM, D_MODEL // TN_OUT),
        in_specs=[
            pl.BlockSpec((TM, ATTN_V), lambda i, j: (i, 0)),
            pl.BlockSpec((TM, DN_V), lambda i, j: (i, 0)),
            pl.BlockSpec((ATTN_V, TN_OUT), lambda i, j: (0, j)),
            pl.BlockSpec((DN_V, TN_OUT), lambda i, j: (1, j)),
            pl.BlockSpec((TM, TN_OUT), lambda i, j: (i, j)),
            pl.BlockSpec((1, 1, TN_OUT), lambda i, j: (i, 0, j)),
        ],
        out_specs=pl.BlockSpec((TM, TN_OUT), lambda i, j: (i, j)),
        out_shape=jax.ShapeDtypeStruct((tokens, D_MODEL), F32),
        compiler_params=pltpu.CompilerParams(
            dimension_semantics=("arbitrary", "arbitrary"), vmem_limit_bytes=VMEM_LIMIT),
        name="out_proj",
    )(o_a, o_d, w_out, w_out, x, gate)


def _ffn_kernel(x_ref, g_ref, sc_ref, sh_ref, gate_ref, wg_ref, wv_ref, cwg_ref, cwv_ref, wd_ref, fg_ref,
                o_ref, h_scr, *, seq):
    j = pl.program_id(1)
    tm = x_ref.shape[0]

    @pl.when(j == 0)
    def _():
        _norm_mod_rows(x_ref, g_ref, sc_ref, sh_ref, h_scr)

    pos = lax.broadcasted_iota(jnp.int32, (tm, TF), 0) % seq
    first = pos == 0
    last = pos == seq - 1

    def conv(y, cw_ref):
        y_prev = jnp.where(first, 0.0, pltpu.roll(y, 1, 0))
        y_next = jnp.where(last, 0.0, pltpu.roll(y, tm - 1, 0))
        cw = cw_ref[...]
        return y_prev * cw[0:1] + y * cw[1:2] + y_next * cw[2:3]

    hb = h_scr[...]
    u_gate = conv(_dot(hb, wg_ref[...].astype(BF16)), cwg_ref)
    u_val = conv(_dot(hb, wv_ref[...].astype(BF16)), cwv_ref)
    act = (_silu(u_gate) * u_val).astype(BF16)
    contrib = _dot(act, wd_ref[...].astype(BF16))

    @pl.when(j == 0)
    def _():
        o_ref[...] = contrib

    @pl.when(j > 0)
    def _():
        o_ref[...] += contrib

    @pl.when(j == pl.num_programs(1) - 1)
    def _():
        gate = gate_ref[0]
        fg = fg_ref[...]

        def body(r, carry):
            rs = pl.multiple_of(r * ROW_CHUNK, ROW_CHUNK)
            x = x_ref[pl.ds(rs, ROW_CHUNK), :] + gate * o_ref[pl.ds(rs, ROW_CHUNK), :]
            ms = jnp.mean(x * x, axis=-1, keepdims=True)
            o_ref[pl.ds(rs, ROW_CHUNK), :] = x * lax.rsqrt(ms + EPS) * fg
            return carry

        lax.fori_loop(0, tm // ROW_CHUNK, body, 0)


def _conv_ffn(x, g, sc, sh, gate, w_up, conv_w, w_down, final_g, seq):
    tokens = x.shape[0]
    n_f = D_FF // TF
    once = pl.Buffered(1)
    return pl.pallas_call(
        functools.partial(_ffn_kernel, seq=seq),
        grid=(tokens // TM, n_f),
        in_specs=[
            pl.BlockSpec((TM, D_MODEL), lambda i, j: (i, 0), pipeline_mode=once),
            pl.BlockSpec((1, D_MODEL), lambda i, j: (0, 0)),
            pl.BlockSpec((1, 1, D_MODEL), lambda i, j: (i, 0, 0)),
            pl.BlockSpec((1, 1, D_MODEL), lambda i, j: (i, 0, 0)),
            pl.BlockSpec((1, 1, D_MODEL), lambda i, j: (i, 0, 0)),
            pl.BlockSpec((D_MODEL, TF), lambda i, j: (0, j)),
            pl.BlockSpec((D_MODEL, TF), lambda i, j: (0, n_f + j)),
            pl.BlockSpec((3, TF), lambda i, j: (0, j)),
            pl.BlockSpec((3, TF), lambda i, j: (0, n_f + j)),
            pl.BlockSpec((TF, D_MODEL), lambda i, j: (j, 0)),
            pl.BlockSpec((1, D_MODEL), lambda i, j: (0, 0)),
        ],
        out_specs=pl.BlockSpec((TM, D_MODEL), lambda i, j: (i, 0), pipeline_mode=once),
        out_shape=jax.ShapeDtypeStruct((tokens, D_MODEL), F32),
        scratch_shapes=[pltpu.VMEM((TM, D_MODEL), BF16)],
        compiler_params=pltpu.CompilerParams(
            dimension_semantics=("arbitrary", "arbitrary"), vmem_limit_bytes=VMEM_LIMIT),
        name="conv_ffn",
    )(x, g, sc, sh, gate, w_up, w_up, conv_w, conv_w, w_down, final_g)


def _layer(x3, mods, p, ctx):
    batch, seq, _ = x3.shape
    x = x3.reshape(batch * seq, D_MODEL)
    sh1, sc1, g1, sh2, sc2, g2 = mods
    z, zs = _in_proj(x, p["norm_mix_g"], sc1, sh1, p["w_in"], p["w_tail"])
    if ctx is None:
        o_a = _attention(z, batch, seq, p["lams"], p["subln_g"])
        o_d, s_f, s_b = _deltanet(z, zs, batch, seq, p["conv_qkv_w"], p["a_log"], p["dt_bias"], p["dn_norm_g"])
    else:
        ck, cv, s0f, s0b = ctx
        o_a = _attention(z, batch, seq, p["lams"], p["subln_g"], ctx=(ck, cv) + _rope_tables(seq))
        o_d = _deltanet(z, zs, batch, seq, p["conv_qkv_w"], p["a_log"], p["dt_bias"], p["dn_norm_g"],
                        init=(s0f, s0b))
        s_f = s_b = None
    x1 = _out_proj(o_a, o_d, p["w_out"], x, g1)
    y = _conv_ffn(x1, p["norm_ffn_g"], sc2, sh2, g2, p["w_up"], p["conv_ffn_w"], p["w_down"], p["final_g"], seq)
    return y.reshape(batch, seq, D_MODEL), z, s_f, s_b


def kernel(x_prompt, x_sample, cache_k, cache_v, state_fwd, state_bwd, c, c_ctx, mod_w, mod_b, norm_mix_g, w_in,
           conv_qkv_w, lambda_q1, lambda_k1, lambda_q2, lambda_k2, subln_g, a_log, dt_bias, dn_norm_g, w_out,
           norm_ffn_g, w_up, conv_ffn_w, w_down, final_g):
    batch, seq, _ = x_prompt.shape
    dec_batch, dec_seq, _ = x_sample.shape
    past = cache_k.shape[2]
    layer = 0

    cvecs = jnp.concatenate([c_ctx[None, :], c, jnp.zeros((8 - 1 - dec_batch, D_MODEL), F32)], axis=0)
    m = _modulation(cvecs, mod_w[layer], mod_b[layer][None, :]).reshape(8, 6, 1, D_MODEL)
    blocks_p = batch * seq // TM
    per_seq = dec_seq // TM
    mods_p = [jnp.broadcast_to(m[0:1, k], (blocks_p, 1, D_MODEL)) for k in range(6)]
    mods_s = [jnp.repeat(m[1:1 + dec_batch, k], per_seq, axis=0) for k in range(6)]

    p = {
        "norm_mix_g": norm_mix_g[layer][None, :],
        "w_in": w_in[layer],
        "w_tail": w_in[layer][:, N_MAIN:],
        "conv_qkv_w": conv_qkv_w[layer],
        "lams": [v[layer][None, :] for v in (lambda_q1, lambda_k1, lambda_q2, lambda_k2)],
        "subln_g": subln_g[layer][None, :],
        "a_log": a_log[layer],
        "dt_bias": dt_bias[layer],
        "dn_norm_g": dn_norm_g[layer][None, :],
        "w_out": w_out[layer],
        "norm_ffn_g": norm_ffn_g[layer][None, :],
        "w_up": w_up[layer],
        "conv_ffn_w": conv_ffn_w[layer],
        "w_down": w_down[layer],
        "final_g": final_g[None, :],
    }

    y_prompt, z_p, s_f, s_b = _layer(x_prompt, mods_p, p, None)
    ctx = (cache_k[:, layer].reshape(dec_batch, past, ATTN_Q), cache_v[:, layer].reshape(dec_batch, past, ATTN_V),
           state_fwd[:, layer], state_bwd[:, layer])
    y_sample, _, _, _ = _layer(x_sample, mods_s, p, ctx)

    new_cache_k = z_p[:, ATTN_Q:2 * ATTN_Q].reshape(batch, 1, seq, H_A, 2 * QK_A)
    new_cache_v = z_p[:, 2 * ATTN_Q:2 * ATTN_Q + ATTN_V].reshape(batch, 1, seq, H_A, V_A)
    return (y_prompt, y_sample, new_cache_k, new_cache_v, s_f[:, None], s_b[:, None])
```
